```python
import jax, jax.numpy as jnp
from jax import lax
import numpy as np

D_MODEL = 1024
BATCH = 4
SEQ = 8192
DEPTH = 4

GRID_W = 64
CTX_LEN = 256
N_MIXERS = 4
GROUP_W = D_MODEL // N_MIXERS
HEAD_DIM = 64
N_Q_HEADS = GROUP_W // HEAD_DIM
N_KV_HEADS = N_Q_HEADS // 2
GQA_GROUP = N_Q_HEADS // N_KV_HEADS
KV_W = N_KV_HEADS * HEAD_DIM
AXIS_DIM = HEAD_DIM // 2
ROPE_THETA = 10000.0
ATTN_SCALE = HEAD_DIM ** -0.5
Q_BLOCK = 128
SHORT_CONV_K = 3
CONFORMER_K = 31
CHUNK = 128
N_SPATIAL_GROUPS = 4
RMS_EPS = 1e-6
LN_EPS = 1e-5

PROJ_WIDTHS = (GROUP_W, GROUP_W, GROUP_W, GROUP_W,
               2 * GROUP_W, GROUP_W,
               GROUP_W, GROUP_W, GROUP_W,
               GROUP_W, KV_W, KV_W, GROUP_W)
PROJ_W = sum(PROJ_WIDTHS)
SPLIT_IDX = tuple(int(i) for i in np.cumsum(PROJ_WIDTHS)[:-1])
KV_START = SPLIT_IDX[9]
KV_END = SPLIT_IDX[11]

kernel_name = "hybrid_parallel_group_dit_trunk"


def rms_norm(x, g):
    xf = x.astype(jnp.float32)
    y = xf * lax.rsqrt(jnp.mean(xf * xf, axis=-1, keepdims=True) + RMS_EPS)
    return (y * g.astype(jnp.float32)).astype(x.dtype)


def layer_norm(x, g, b):
    xf = x.astype(jnp.float32)
    mu = jnp.mean(xf, axis=-1, keepdims=True)
    xc = xf - mu
    y = xc * lax.rsqrt(jnp.mean(xc * xc, axis=-1, keepdims=True) + LN_EPS)
    return (y * g.astype(jnp.float32) + b.astype(jnp.float32)).astype(x.dtype)


def dwconv(x, w):
    k = w.shape[0]
    return lax.conv_general_dilated(
        x, w[:, None, :].astype(x.dtype), window_strides=(1,), padding=[(k // 2, k // 2)],
        dimension_numbers=('NWC', 'WIO', 'NWC'), feature_group_count=x.shape[-1])


def short_conv_mixer(b_gate, c_gate, h, w):
    return b_gate * dwconv(c_gate * h, w)


def conformer_conv_mixer(glu_in, w, bias, ln_g, ln_b):
    a, g = jnp.split(glu_in, 2, axis=-1)
    z = a * jax.nn.sigmoid(g)
    z = dwconv(z, w) + bias
    return jax.nn.silu(layer_norm(z, ln_g, ln_b))


def chunk_mlp_mixer(u, v, ln_g, ln_b, w_s, b_s):
    bsz, s, ch = v.shape
    v = layer_norm(v, ln_g, ln_b)
    vc = v.reshape(bsz, s // CHUNK, CHUNK, N_SPATIAL_GROUPS, ch // N_SPATIAL_GROUPS)
    sg = jnp.einsum('gij,bnjgc->bnigc', w_s.astype(v.dtype), vc) + b_s.T[:, :, None].astype(v.dtype)
    return u * sg.reshape(bsz, s, ch)


def rope_axis(x, cos, sin):
    x1, x2 = jnp.split(x, 2, axis=-1)
    return jnp.concatenate([x1 * cos - x2 * sin, x2 * cos + x1 * sin], axis=-1)


def rope_2d(x, cos_r, sin_r, cos_c, sin_c):
    xr, xc = jnp.split(x, 2, axis=-1)
    return jnp.concatenate([rope_axis(xr, cos_r, sin_r), rope_axis(xc, cos_c, sin_c)], axis=-1)


def split_heads(t, n_heads):
    return t.reshape(t.shape[:-1] + (n_heads, HEAD_DIM))


def attend(q, k, v):
    bsz, nq = q.shape[:2]
    qg = q.reshape(bsz, nq, N_KV_HEADS, GQA_GROUP, HEAD_DIM)
    s = jnp.einsum('bqkgd,btkd->bkgqt', qg, k).astype(jnp.float32) * ATTN_SCALE
    p = jax.nn.softmax(s, axis=-1).astype(v.dtype)
    o = jnp.einsum('bkgqt,btkd->bqkgd', p, v)
    return o.reshape(bsz, nq, N_Q_HEADS * HEAD_DIM)


def attend_blocks(q, k, v):
    bsz, s = q.shape[:2]
    nb = s // Q_BLOCK
    qb = q.reshape(bsz, nb, Q_BLOCK, N_Q_HEADS, HEAD_DIM).swapaxes(0, 1)
    o = lax.map(lambda qi: attend(qi, k, v), qb)
    return o.swapaxes(0, 1).reshape(bsz, s, N_Q_HEADS * HEAD_DIM)


def mixer_output(parts, att, conv_a_l, conv_b_l, conv_b_bias_l, conf_ln_g_l, conf_ln_b_l,
                 sgu_ln_g_l, sgu_ln_b_l, w_s_l, b_s_l, w_out_l):
    a_b, a_c, a_h, a_g, b_glu, b_g, c_u, c_v, c_g = parts[:9]
    d_g = parts[12]
    ya = short_conv_mixer(a_b, a_c, a_h, conv_a_l)
    yb = conformer_conv_mixer(b_glu, conv_b_l, conv_b_bias_l, conf_ln_g_l, conf_ln_b_l)
    yc = chunk_mlp_mixer(c_u, c_v, sgu_ln_g_l, sgu_ln_b_l, w_s_l, b_s_l)
    y = jnp.concatenate([ya * jax.nn.silu(a_g), yb * jax.nn.silu(b_g),
                         yc * jax.nn.silu(c_g), att * jax.nn.silu(d_g)], axis=-1)
    return y @ w_out_l


def setup_inputs(seed: int = 0) -> dict:
    key = jax.random.key(seed)
    ks = jax.random.split(key, 24)
    f32 = jnp.float32
    nrm = lambda k, shape, s: jax.random.normal(k, shape, f32) * s
    return {
        "x": nrm(ks[0], (BATCH, SEQ, D_MODEL), 1.0),
        "c": nrm(ks[1], (BATCH, D_MODEL), 1.0),
        "ctx": nrm(ks[2], (BATCH, CTX_LEN, D_MODEL), 1.0),
        "c_ctx": nrm(ks[3], (D_MODEL,), 1.0),
        "w_mod": nrm(ks[4], (DEPTH, D_MODEL, 3 * D_MODEL), D_MODEL ** -0.5),
        "b_mod": nrm(ks[5], (DEPTH, 3 * D_MODEL), 0.02),
        "g_pre": 1.0 + nrm(ks[6], (DEPTH, D_MODEL), 0.02),
        "g_post": 1.0 + nrm(ks[7], (DEPTH, D_MODEL), 0.02),
        "w_in": nrm(ks[8], (DEPTH, D_MODEL, PROJ_W), D_MODEL ** -0.5),
        "w_out": nrm(ks[9], (DEPTH, D_MODEL, D_MODEL), D_MODEL ** -0.5),
        "conv_a": nrm(ks[10], (DEPTH, SHORT_CONV_K, GROUP_W), SHORT_CONV_K ** -0.5),
        "conv_b": nrm(ks[11], (DEPTH, CONFORMER_K, GROUP_W), CONFORMER_K ** -0.5),
        "conv_b_bias": nrm(ks[12], (DEPTH, GROUP_W), 0.02),
        "conf_ln_g": 1.0 + nrm(ks[13], (DEPTH, GROUP_W), 0.02),
        "conf_ln_b": nrm(ks[14], (DEPTH, GROUP_W), 0.02),
        "sgu_ln_g": 1.0 + nrm(ks[15], (DEPTH, GROUP_W), 0.02),
        "sgu_ln_b": nrm(ks[16], (DEPTH, GROUP_W), 0.02),
        "w_s": nrm(ks[17], (DEPTH, N_SPATIAL_GROUPS, CHUNK, CHUNK), CHUNK ** -0.5),
        "b_s": 1.0 + nrm(ks[18], (DEPTH, N_SPATIAL_GROUPS, CHUNK), 0.1),
        "q_gain": 1.0 + nrm(ks[19], (DEPTH, HEAD_DIM), 0.02),
        "k_gain": 1.0 + nrm(ks[20], (DEPTH, HEAD_DIM), 0.02),
    }


def reference(x, c, ctx, c_ctx, w_mod, b_mod, g_pre, g_post, w_in, w_out, conv_a, conv_b, conv_b_bias,
              conf_ln_g, conf_ln_b, sgu_ln_g, sgu_ln_b, w_s, b_s, q_gain, k_gain):
    bsz, s, _ = x.shape
    rows = s // GRID_W
    r_idx, c_idx = jnp.meshgrid(jnp.arange(rows), jnp.arange(GRID_W), indexing='ij')
    pos_row = r_idx.reshape(-1).astype(jnp.float32)
    pos_col = c_idx.reshape(-1).astype(jnp.float32)
    inv_freq = 1.0 / (ROPE_THETA ** (jnp.arange(0, AXIS_DIM, 2, dtype=jnp.float32) / AXIS_DIM))
    ang_r = pos_row[:, None] * inv_freq[None, :]
    ang_c = pos_col[:, None] * inv_freq[None, :]
    cos_r = jnp.cos(ang_r)[:, None, :].astype(x.dtype)
    sin_r = jnp.sin(ang_r)[:, None, :].astype(x.dtype)
    cos_c = jnp.cos(ang_c)[:, None, :].astype(x.dtype)
    sin_c = jnp.sin(ang_c)[:, None, :].astype(x.dtype)

    xc = ctx
    for l in range(DEPTH):
        last = l == DEPTH - 1
        mod_l = (jax.nn.silu(c) @ w_mod[l] + b_mod[l])[:, None, :]
        mod_c = (jax.nn.silu(c_ctx) @ w_mod[l] + b_mod[l])[None, None, :]
        sh, sc, gt = jnp.split(mod_l, 3, axis=-1)
        sh_c, sc_c, gt_c = jnp.split(mod_c, 3, axis=-1)
        h = rms_norm(x, g_pre[l]) * (1.0 + sc) + sh
        hc = rms_norm(xc, g_pre[l]) * (1.0 + sc_c) + sh_c

        parts = jnp.split(h @ w_in[l], SPLIT_IDX, axis=-1)
        if last:
            k_c, v_c = jnp.split(hc @ w_in[l][:, KV_START:KV_END], 2, axis=-1)
            parts_c = None
        else:
            parts_c = jnp.split(hc @ w_in[l], SPLIT_IDX, axis=-1)
            k_c, v_c = parts_c[10], parts_c[11]
        k_c = rms_norm(split_heads(k_c, N_KV_HEADS), k_gain[l])
        v_c = split_heads(v_c, N_KV_HEADS)

        q = rope_2d(rms_norm(split_heads(parts[9], N_Q_HEADS), q_gain[l]), cos_r, sin_r, cos_c, sin_c)
        k = rope_2d(rms_norm(split_heads(parts[10], N_KV_HEADS), k_gain[l]), cos_r, sin_r, cos_c, sin_c)
        v = split_heads(parts[11], N_KV_HEADS)
        k_all = jnp.concatenate([k_c, k], axis=1)
        v_all = jnp.concatenate([v_c, v], axis=1)
        att = attend_blocks(q, k_all, v_all)

        y = mixer_output(parts, att, conv_a[l], conv_b[l], conv_b_bias[l], conf_ln_g[l], conf_ln_b[l],
                         sgu_ln_g[l], sgu_ln_b[l], w_s[l], b_s[l], w_out[l])

        if not last:
            q_c = rms_norm(split_heads(parts_c[9], N_Q_HEADS), q_gain[l])
            att_c = attend(q_c, k_c, v_c)
            y_c = mixer_output(parts_c, att_c, conv_a[l], conv_b[l], conv_b_bias[l], conf_ln_g[l], conf_ln_b[l],
                               sgu_ln_g[l], sgu_ln_b[l], w_s[l], b_s[l], w_out[l])
            xc = xc + gt_c * rms_norm(y_c, g_post[l])

        x = x + gt * rms_norm(y, g_post[l])
    return x
```

```python
import functools

import jax
import jax.numpy as jnp
from jax import lax
from jax.experimental import pallas as pl
from jax.experimental.pallas import tpu as pltpu

F32 = jnp.float32
BF16 = jnp.bfloat16

GRID_W = 64
GROUP_W = 256
HEAD_DIM = 64
N_Q_HEADS = 4
N_KV_HEADS = 2
KV_W = N_KV_HEADS * HEAD_DIM
AXIS_DIM = HEAD_DIM // 2
ROPE_THETA = 10000.0
ATTN_SCALE = HEAD_DIM ** -0.5
SHORT_CONV_K = 3
CONFORMER_K = 31
CHUNK = 128
N_SPATIAL_GROUPS = 4
RMS_EPS = 1e-6
LN_EPS = 1e-5

COL_A = 0
COL_B = 4 * GROUP_W
COL_C = COL_B + 3 * GROUP_W
COL_D = COL_C + 3 * GROUP_W
PROJ_W = COL_D + 2 * GROUP_W + 2 * KV_W

LANES = 128
BF16_SUBLANES = 16
HALO = BF16_SUBLANES
VMEM_LIMIT_BYTES = 56 * 1024 * 1024

ROW_TILE = 512
Q_TILE = 256
KEY_TILE = ROW_TILE


def _silu(a):
    return a * jax.nn.sigmoid(a)


def _dot(a, b):
    return jnp.dot(a, b, preferred_element_type=F32)


def _split_bf16(a):
    hi = a.astype(BF16)
    lo = (a - hi.astype(F32)).astype(BF16)
    return hi, lo


def _mod_kernel(c_ref, w_ref, b_ref, o_ref):
    a_hi, a_lo = _split_bf16(_silu(c_ref[...]))
    w_hi, w_lo = _split_bf16(w_ref[0])
    o_ref[0] = _dot(a_hi, w_hi) + _dot(a_lo, w_hi) + _dot(a_hi, w_lo) + b_ref[0]


def _modulation(cc, w_mod, b_mod):
    depth, d, d3 = w_mod.shape
    tn = d
    return pl.pallas_call(
        _mod_kernel,
        grid=(depth, d3 // tn),
        in_specs=[
            pl.BlockSpec((cc.shape[0], d), lambda l, j: (0, 0)),
            pl.BlockSpec((1, d, tn), lambda l, j: (l, 0, j)),
            pl.BlockSpec((1, 1, tn), lambda l, j: (l, 0, j)),
        ],
        out_specs=pl.BlockSpec((1, cc.shape[0], tn), lambda l, j: (l, 0, j)),
        out_shape=jax.ShapeDtypeStruct((depth, cc.shape[0], d3), F32),
        compiler_params=pltpu.CompilerParams(vmem_limit_bytes=VMEM_LIMIT_BYTES),
        name="adaln_modulation",
    )(cc, w_mod, b_mod.reshape(depth, 1, d3))


def _in_kernel(x_ref, mod_ref, gpre_ref, w_ref, cos_ref, sin_ref, qg_ref, kg_ref, bd_ref, lng_ref, lnb_ref,
               t_ref, pa_ref, z_ref, sb_ref, vn_ref, uc_ref, q_ref, k_ref, vt_ref, sd_ref):
    x = x_ref[0]
    ms = jnp.mean(x * x, axis=-1, keepdims=True)
    y = x * lax.rsqrt(ms + RMS_EPS) * gpre_ref[...]
    sh = mod_ref[0, 0:1, :]
    sc = mod_ref[0, 1:2, :]
    h = (y * (1.0 + sc) + sh).astype(BF16)

    def proj(lo, width):
        return _dot(h, w_ref[:, lo:lo + width])

    g = GROUP_W
    p = proj(COL_A, 4 * g)
    t_ref[0] = (p[:, g:2 * g] * p[:, 2 * g:3 * g]).astype(BF16)
    pa_ref[0] = (p[:, 0:g] * _silu(p[:, 3 * g:4 * g])).astype(BF16)

    p = proj(COL_B, 3 * g)
    z_ref[0] = (p[:, 0:g] * jax.nn.sigmoid(p[:, g:2 * g])).astype(BF16)
    sb_ref[0] = _silu(p[:, 2 * g:3 * g]).astype(BF16)

    p = proj(COL_C, 3 * g)
    cv = p[:, g:2 * g]
    mu = jnp.mean(cv, axis=-1, keepdims=True)
    cvc = cv - mu
    var = jnp.mean(cvc * cvc, axis=-1, keepdims=True)
    vn_ref[0] = (cvc * lax.rsqrt(var + LN_EPS) * lng_ref[...] + lnb_ref[...]).astype(BF16)
    uc_ref[0] = (p[:, 0:g] * _silu(p[:, 2 * g:3 * g])).astype(BF16)

    p = proj(COL_D, 2 * g + 2 * KV_W)
    bd = bd_ref[...]
    cos = cos_ref[...]
    sin = sin_ref[...]
    lane = lax.broadcasted_iota(jnp.int32, (x.shape[0], LANES), 1)
    first_half = (lane % (2 * (AXIS_DIM // 2))) < (AXIS_DIM // 2)

    def norm_rope(u, gain):
        hi, lo = _split_bf16(u * u)
        msq = _dot(hi, bd) + _dot(lo, bd)
        u = u * lax.rsqrt(msq + RMS_EPS) * gain
        partner = jnp.where(first_half,
                            pltpu.roll(u, LANES - AXIS_DIM // 2, 1),
                            pltpu.roll(u, AXIS_DIM // 2, 1))
        return u * cos + partner * sin

    qg = qg_ref[...]
    q0 = norm_rope(p[:, 0:LANES], qg) * ATTN_SCALE
    q1 = norm_rope(p[:, LANES:2 * LANES], qg) * ATTN_SCALE
    q_ref[0] = jnp.concatenate([q0, q1], axis=1).astype(BF16)
    k_ref[0] = norm_rope(p[:, g:g + KV_W], kg_ref[...]).astype(BF16)
    vt_ref[0, 0] = p[:, g + KV_W:g + 2 * KV_W].T.astype(BF16)
    sd_ref[0] = _silu(p[:, g + 2 * KV_W:2 * g + 2 * KV_W]).astype(BF16)


def _in_projection(x, mod, g_pre, w_in, cos_t, sin_t, qg, kg, bd, ln_g, ln_b, tm):
    bsz, s, d = x.shape
    nt = s // tm
    row = lambda width: pl.BlockSpec((1, tm, width), lambda b, i: (b, i, 0))
    full = lambda a: pl.BlockSpec(a.shape, lambda b, i: (0,) * a.ndim)
    act = lambda width: jax.ShapeDtypeStruct((bsz, s, width), BF16)
    g = GROUP_W
    return pl.pallas_call(
        _in_kernel,
        grid=(bsz, nt),
        in_specs=[
            row(d),
            pl.BlockSpec((1, 3, d), lambda b, i: (b, 0, 0)),
            full(g_pre), full(w_in),
            pl.BlockSpec((tm, LANES), lambda b, i: (i, 0)),
            pl.BlockSpec((tm, LANES), lambda b, i: (i, 0)),
            full(qg), full(kg), full(bd), full(ln_g), full(ln_b),
        ],
        out_specs=[row(g), row(g), row(g), row(g), row(g), row(g), row(g), row(KV_W),
                   pl.BlockSpec((1, 1, KV_W, tm), lambda b, i: (b, i, 0, 0)),
                   row(g)],
        out_shape=[act(g), act(g), act(g), act(g), act(g), act(g), act(g), act(KV_W),
                   jax.ShapeDtypeStruct((bsz, nt, KV_W, tm), BF16),
                   act(g)],
        compiler_params=pltpu.CompilerParams(vmem_limit_bytes=VMEM_LIMIT_BYTES),
        name="in_projection",
    )(x, mod, g_pre, w_in, cos_t, sin_t, qg, kg, bd, ln_g, ln_b)


def _attn_kernel(*refs, n_lat_tiles, key_tile):
    if n_lat_tiles:
        q_ref, kc_ref, vtc_ref, k_ref, vt_ref, sd_ref, o_ref = refs
    else:
        q_ref, kc_ref, vtc_ref, sd_ref, o_ref = refs
    q = q_ref[0]
    tq = q.shape[0]
    lane = lax.broadcasted_iota(jnp.int32, (tq, LANES), 1)
    heads_out = []
    for h in range(N_Q_HEADS):
        kv, grp = divmod(h, N_Q_HEADS // N_KV_HEADS)
        qh = jnp.where((lane // HEAD_DIM) == kv, q[:, grp * LANES:(grp + 1) * LANES], jnp.zeros((), BF16))

        def step(carry, k_tile, vt_tile, qh=qh):
            m, l, acc = carry
            s = lax.dot_general(k_tile, qh, (((1,), (1,)), ((), ())), preferred_element_type=F32)
            m_new = jnp.maximum(m, jnp.max(s, axis=0, keepdims=True))
            alpha = jnp.exp(m - m_new)
            p = jnp.exp(s - m_new)
            l = alpha * l + jnp.sum(p, axis=0, keepdims=True)
            acc = alpha * acc + _dot(vt_tile, p.astype(BF16))
            return m_new, l, acc

        lo, hi = kv * HEAD_DIM, (kv + 1) * HEAD_DIM
        carry = (jnp.full((1, tq), -1e30, F32), jnp.zeros((1, tq), F32), jnp.zeros((HEAD_DIM, tq), F32))
        carry = step(carry, kc_ref[0], vtc_ref[0, 0, lo:hi, :])
        if n_lat_tiles:
            def body(j, c, lo=lo, hi=hi, step=step):
                start = pl.multiple_of(j * key_tile, key_tile)
                return step(c, k_ref[0, pl.ds(start, key_tile), :], vt_ref[0, j, lo:hi, :])
            carry = lax.fori_loop(0, n_lat_tiles, body, carry)
        _, l, acc = carry
        heads_out.append(acc / l)
    o = jnp.concatenate(heads_out, axis=0).T
    o_ref[0] = (o * sd_ref[0].astype(F32)).astype(BF16)


def _attention(q, kc, vtc, k, vt, sd):
    bsz, s, g = q.shape
    ctx_len = kc.shape[1]
    tq = min(Q_TILE, s)
    qspec = pl.BlockSpec((1, tq, g), lambda b, i: (b, i, 0))
    in_specs = [qspec,
                pl.BlockSpec((1, ctx_len, KV_W), lambda b, i: (b, 0, 0)),
                pl.BlockSpec((1, 1, KV_W, ctx_len), lambda b, i: (b, 0, 0, 0))]
    args = [q, kc, vtc]
    n_lat_tiles, key_tile = 0, 0
    if k is not None:
        n_lat_tiles, key_tile = vt.shape[1], vt.shape[3]
        in_specs += [pl.BlockSpec((1, k.shape[1], KV_W), lambda b, i: (b, 0, 0)),
                     pl.BlockSpec((1, n_lat_tiles, KV_W, key_tile), lambda b, i: (b, 0, 0, 0))]
        args += [k, vt]
    in_specs.append(qspec)
    args.append(sd)
    return pl.pallas_call(
        functools.partial(_attn_kernel, n_lat_tiles=n_lat_tiles, key_tile=key_tile),
        grid=(bsz, s // tq),
        in_specs=in_specs,
        out_specs=qspec,
        out_shape=jax.ShapeDtypeStruct((bsz, s, g), BF16),
        compiler_params=pltpu.CompilerParams(vmem_limit_bytes=VMEM_LIMIT_BYTES),
        name="attention" if k is not None else "attention_ctx",
    )(*args)


def _layer_norm(v, g, b):
    mu = jnp.mean(v, axis=-1, keepdims=True)
    vc = v - mu
    var = jnp.mean(vc * vc, axis=-1, keepdims=True)
    return vc * lax.rsqrt(var + LN_EPS) * g + b


def _out_kernel(t_ref, tp_ref, tn_ref, z_ref, zp_ref, zn_ref, pa_ref, sb_ref, vn_ref, uc_ref, yd_ref,
                x_ref, mod_ref, ca_ref, cb_ref, cbb_ref, clg_ref, clb_ref, wcat_ref, bias_ref, wout_ref, gpost_ref,
                o_ref, text_ref, zext_ref):
    i = pl.program_id(1)
    tm = t_ref.shape[1]
    has_prev = (i > 0).astype(F32)
    has_next = (i < pl.num_programs(1) - 1).astype(F32)

    def extend(ext_ref, main_ref, prev_ref, next_ref):
        ext_ref[0:HALO, :] = prev_ref[0].astype(F32) * has_prev
        ext_ref[HALO:HALO + tm, :] = main_ref[0].astype(F32)
        ext_ref[HALO + tm:2 * HALO + tm, :] = next_ref[0].astype(F32) * has_next

    def dwconv(ext_ref, w_ref):
        taps = w_ref.shape[0]
        first = HALO - taps // 2
        acc = ext_ref[first:first + tm, :] * w_ref[0:1, :]
        for k in range(1, taps):
            acc = acc + ext_ref[first + k:first + k + tm, :] * w_ref[k:k + 1, :]
        return acc

    extend(text_ref, t_ref, tp_ref, tn_ref)
    extend(zext_ref, z_ref, zp_ref, zn_ref)

    ya = pa_ref[0].astype(F32) * dwconv(text_ref, ca_ref)
    zc = dwconv(zext_ref, cb_ref) + cbb_ref[...]
    yb = _silu(_layer_norm(zc, clg_ref[...], clb_ref[...])) * sb_ref[0].astype(F32)

    lane_group = lax.broadcasted_iota(jnp.int32, (CHUNK, GROUP_W), 1) // (GROUP_W // N_SPATIAL_GROUPS)
    wcat = wcat_ref[...]
    bias = bias_ref[...]
    yc_chunks = []
    for c in range(tm // CHUNK):
        vchunk = vn_ref[0, c * CHUNK:(c + 1) * CHUNK, :]
        stacked = jnp.concatenate(
            [jnp.where(lane_group == grp, vchunk, jnp.zeros((), BF16)) for grp in range(N_SPATIAL_GROUPS)], axis=0)
        sg = _dot(wcat, stacked) + bias
        yc_chunks.append(sg * uc_ref[0, c * CHUNK:(c + 1) * CHUNK, :].astype(F32))
    yc = jnp.concatenate(yc_chunks, axis=0)

    y = jnp.concatenate([ya.astype(BF16), yb.astype(BF16), yc.astype(BF16), yd_ref[0]], axis=1)
    out = _dot(y, wout_ref[...])
    ms = jnp.mean(out * out, axis=-1, keepdims=True)
    yn = out * lax.rsqrt(ms + RMS_EPS) * gpost_ref[...]
    o_ref[0] = x_ref[0] + mod_ref[0, 2:3, :] * yn


def _mixer_output(t, pa, z, sb, vn, uc, yd, x, mod, conv_a, conv_b, conv_b_bias, conf_ln_g, conf_ln_b,
                  wcat, bias2d, w_out, g_post, tm):
    bsz, s, d = x.shape
    nt = s // tm
    hb = tm // HALO
    last_hb = s // HALO - 1
    row = lambda width: pl.BlockSpec((1, tm, width), lambda b, i: (b, i, 0))
    prev = pl.BlockSpec((1, HALO, GROUP_W), lambda b, i: (b, jnp.maximum(i * hb - 1, 0), 0))
    nxt = pl.BlockSpec((1, HALO, GROUP_W), lambda b, i: (b, jnp.minimum((i + 1) * hb, last_hb), 0))
    full = lambda a: pl.BlockSpec(a.shape, lambda b, i: (0,) * a.ndim)
    g = GROUP_W
    return pl.pallas_call(
        _out_kernel,
        grid=(bsz, nt),
        in_specs=[row(g), prev, nxt, row(g), prev, nxt, row(g), row(g), row(g), row(g), row(g),
                  row(d), pl.BlockSpec((1, 3, d), lambda b, i: (b, 0, 0)),
                  full(conv_a), full(conv_b), full(conv_b_bias), full(conf_ln_g), full(conf_ln_b),
                  full(wcat), full(bias2d), full(w_out), full(g_post)],
        out_specs=row(d),
        out_shape=jax.ShapeDtypeStruct((bsz, s, d), F32),
        scratch_shapes=[pltpu.VMEM((tm + 2 * HALO, g), F32), pltpu.VMEM((tm + 2 * HALO, g), F32)],
        compiler_params=pltpu.CompilerParams(vmem_limit_bytes=VMEM_LIMIT_BYTES),
        name="mixer_output",
    )(t, t, t, z, z, z, pa, sb, vn, uc, yd, x, mod, conv_a, conv_b, conv_b_bias, conf_ln_g, conf_ln_b,
      wcat, bias2d, w_out, g_post)


def _rope_tables(s):
    pos = jnp.arange(s)
    pos_row = (pos // GRID_W).astype(F32)
    pos_col = (pos % GRID_W).astype(F32)
    inv_freq = 1.0 / (ROPE_THETA ** (jnp.arange(0, AXIS_DIM, 2, dtype=F32) / AXIS_DIM))
    ang_r = pos_row[:, None] * inv_freq[None, :]
    ang_c = pos_col[:, None] * inv_freq[None, :]
    cos_h = jnp.concatenate([jnp.cos(ang_r), jnp.cos(ang_r), jnp.cos(ang_c), jnp.cos(ang_c)], axis=1)
    sin_h = jnp.concatenate([-jnp.sin(ang_r), jnp.sin(ang_r), -jnp.sin(ang_c), jnp.sin(ang_c)], axis=1)
    reps = LANES // HEAD_DIM
    return jnp.tile(cos_h, (1, reps)), jnp.tile(sin_h, (1, reps))


def kernel(x, c, ctx, c_ctx, w_mod, b_mod, g_pre, g_post, w_in, w_out, conv_a, conv_b, conv_b_bias, conf_ln_g,
           conf_ln_b, sgu_ln_g, sgu_ln_b, w_s, b_s, q_gain, k_gain):
    bsz, s, d = x.shape
    depth = w_mod.shape[0]
    ctx_len = ctx.shape[1]
    assert s % ROW_TILE == 0 and ctx_len % CHUNK == 0 and ctx_len % HALO == 0

    n_mod_rows = 8
    cc = jnp.concatenate([c, c_ctx[None, :], jnp.zeros((n_mod_rows - bsz - 1, d), F32)], axis=0)
    mods = _modulation(cc, w_mod, b_mod)

    cos_l, sin_l = _rope_tables(s)
    cos_c = jnp.ones((ctx_len, LANES), F32)
    sin_c = jnp.zeros((ctx_len, LANES), F32)
    head_id = jnp.arange(LANES) // HEAD_DIM
    bd = jnp.where(head_id[:, None] == head_id[None, :], 1.0 / HEAD_DIM, 0.0).astype(BF16)

    head_order = jnp.array([0, 2, 1, 3])
    q_cols = (COL_D + head_order[:, None] * HEAD_DIM + jnp.arange(HEAD_DIM)[None, :]).reshape(-1)
    cols = jnp.concatenate([jnp.arange(COL_D), q_cols, jnp.arange(COL_D + GROUP_W, PROJ_W)])
    w_in_b = jnp.take(w_in, cols, axis=2).astype(BF16)
    w_out_b = w_out.astype(BF16)
    reps = LANES // HEAD_DIM
    wcat = jnp.transpose(w_s, (0, 2, 1, 3)).reshape(depth, CHUNK, N_SPATIAL_GROUPS * CHUNK).astype(BF16)
    bias2d = jnp.repeat(jnp.transpose(b_s, (0, 2, 1)), GROUP_W // N_SPATIAL_GROUPS, axis=2)

    xc = ctx
    for l in range(depth):
        last = l == depth - 1
        mod_l = mods[l, :bsz].reshape(bsz, 3, d)
        mod_c = jnp.broadcast_to(mods[l, bsz].reshape(1, 3, d), (bsz, 3, d))
        qg = jnp.tile(q_gain[l], reps)[None, :]
        kg = jnp.tile(k_gain[l], reps)[None, :]
        shared = (qg, kg, bd, sgu_ln_g[l][None, :], sgu_ln_b[l][None, :])
        gp = g_pre[l][None, :]

        lat = _in_projection(x, mod_l, gp, w_in_b[l], cos_l, sin_l, *shared, tm=ROW_TILE)
        cpr = _in_projection(xc, mod_c, gp, w_in_b[l], cos_c, sin_c, *shared, tm=ctx_len)
        t, pa, z, sb, vn, uc, q, k, vt, sd = lat
        t_c, pa_c, z_c, sb_c, vn_c, uc_c, q_c, k_c, vt_c, sd_c = cpr

        mix = (conv_a[l], conv_b[l], conv_b_bias[l][None, :], conf_ln_g[l][None, :], conf_ln_b[l][None, :],
               wcat[l], bias2d[l], w_out_b[l], g_post[l][None, :])
        yd = _attention(q, k_c, vt_c, k, vt, sd)
        x = _mixer_output(t, pa, z, sb, vn, uc, yd, x, mod_l, *mix, tm=ROW_TILE)
        if not last:
            yd_c = _attention(q_c, k_c, vt_c, None, None, sd_c)
            xc = _mixer_output(t_c, pa_c, z_c, sb_c, vn_c, uc_c, yd_c, xc, mod_c, *mix, tm=ctx_len)
    return x
```

```python
import functools

import jax
import jax.numpy as jnp
from jax import lax
from jax.experimental import pallas as pl
from jax.experimental.pallas import tpu as pltpu

F32 = jnp.float32
BF16 = jnp.bfloat16

GRID_W = 64
GROUP_W = 256
HEAD_DIM = 64
N_Q_HEADS = 4
N_KV_HEADS = 2
KV_W = N_KV_HEADS * HEAD_DIM
AXIS_DIM = HEAD_DIM // 2
ROPE_THETA = 10000.0
ATTN_SCALE = HEAD_DIM ** -0.5
LOG2_E = 1.4426950408889634
Q_SCALE = ATTN_SCALE * LOG2_E
SHORT_CONV_K = 3
CONFORMER_K = 31
CHUNK = 128
N_SPATIAL_GROUPS = 4
RMS_EPS = 1e-6
LN_EPS = 1e-5

COL_A = 0
COL_B = 4 * GROUP_W
COL_C = COL_B + 3 * GROUP_W
COL_D = COL_C + 3 * GROUP_W
PROJ_W = COL_D + 2 * GROUP_W + 2 * KV_W

LANES = 128
BF16_SUBLANES = 16
HALO = BF16_SUBLANES
VMEM_LIMIT_BYTES = 56 * 1024 * 1024

ROW_TILE = 512
Q_TILE = 256
KEY_TILE = 256
V_ROWS = HEAD_DIM + BF16_SUBLANES


def _silu(a):
    return a * jax.nn.sigmoid(a)


def _dot(a, b):
    return jnp.dot(a, b, preferred_element_type=F32)


def _split_bf16(a):
    hi = a.astype(BF16)
    lo = (a - hi.astype(F32)).astype(BF16)
    return hi, lo


def _mod_kernel(c_ref, w_ref, b_ref, o_ref):
    a_hi, a_lo = _split_bf16(_silu(c_ref[...]))
    w_hi, w_lo = _split_bf16(w_ref[0])
    o_ref[0] = _dot(a_hi, w_hi) + _dot(a_lo, w_hi) + _dot(a_hi, w_lo) + b_ref[0]


def _modulation(cc, w_mod, b_mod):
    depth, d, d3 = w_mod.shape
    tn = d
    return pl.pallas_call(
        _mod_kernel,
        grid=(depth, d3 // tn),
        in_specs=[
            pl.BlockSpec((cc.shape[0], d), lambda l, j: (0, 0)),
            pl.BlockSpec((1, d, tn), lambda l, j: (l, 0, j)),
            pl.BlockSpec((1, 1, tn), lambda l, j: (l, 0, j)),
        ],
        out_specs=pl.BlockSpec((1, cc.shape[0], tn), lambda l, j: (l, 0, j)),
        out_shape=jax.ShapeDtypeStruct((depth, cc.shape[0], d3), F32),
        compiler_params=pltpu.CompilerParams(vmem_limit_bytes=VMEM_LIMIT_BYTES),
        name="adaln_modulation",
    )(cc, w_mod, b_mod.reshape(depth, 1, d3))


def _in_kernel(x_ref, mod_ref, gpre_ref, w_ref, cos_ref, sin_ref, qg_ref, kg_ref, bd_ref, lng_ref, lnb_ref,
               t_ref, pa_ref, z_ref, sb_ref, vn_ref, uc_ref, q_ref, k_ref, vt_ref, sd_ref):
    x = x_ref[0]
    ms = jnp.mean(x * x, axis=-1, keepdims=True)
    y = x * lax.rsqrt(ms + RMS_EPS) * gpre_ref[...]
    sh = mod_ref[0, 0:1, :]
    sc = mod_ref[0, 1:2, :]
    h = (y * (1.0 + sc) + sh).astype(BF16)

    def proj(lo, width):
        return _dot(h, w_ref[:, lo:lo + width])

    g = GROUP_W
    p = proj(COL_A, 4 * g)
    t_ref[0] = (p[:, g:2 * g] * p[:, 2 * g:3 * g]).astype(BF16)
    pa_ref[0] = (p[:, 0:g] * _silu(p[:, 3 * g:4 * g])).astype(BF16)

    p = proj(COL_B, 3 * g)
    z_ref[0] = (p[:, 0:g] * jax.nn.sigmoid(p[:, g:2 * g])).astype(BF16)
    sb_ref[0] = _silu(p[:, 2 * g:3 * g]).astype(BF16)

    p = proj(COL_C, 3 * g)
    cv = p[:, g:2 * g]
    mu = jnp.mean(cv, axis=-1, keepdims=True)
    cvc = cv - mu
    var = jnp.mean(cvc * cvc, axis=-1, keepdims=True)
    vn_ref[0] = (cvc * lax.rsqrt(var + LN_EPS) * lng_ref[...] + lnb_ref[...]).astype(BF16)
    uc_ref[0] = (p[:, 0:g] * _silu(p[:, 2 * g:3 * g])).astype(BF16)

    p = proj(COL_D, 2 * g + 2 * KV_W)
    bd = bd_ref[...]
    cos = cos_ref[...]
    sin = sin_ref[...]
    lane = lax.broadcasted_iota(jnp.int32, (x.shape[0], LANES), 1)
    first_half = (lane % (2 * (AXIS_DIM // 2))) < (AXIS_DIM // 2)

    def norm_rope(u, gain):
        hi, lo = _split_bf16(u * u)
        msq = _dot(hi, bd) + _dot(lo, bd)
        u = u * lax.rsqrt(msq + RMS_EPS) * gain
        partner = jnp.where(first_half,
                            pltpu.roll(u, LANES - AXIS_DIM // 2, 1),
                            pltpu.roll(u, AXIS_DIM // 2, 1))
        return u * cos + partner * sin

    qg = qg_ref[...]
    q0 = norm_rope(p[:, 0:LANES], qg) * Q_SCALE
    q1 = norm_rope(p[:, LANES:2 * LANES], qg) * Q_SCALE
    q_ref[0] = jnp.concatenate([q0, q1], axis=1).astype(BF16)
    k_ref[0] = norm_rope(p[:, g:g + KV_W], kg_ref[...]).astype(BF16)
    vt = p[:, g + KV_W:g + 2 * KV_W].T
    sub = lax.broadcasted_iota(jnp.int32, (V_ROWS - HEAD_DIM, x.shape[0]), 0)
    ones_row = jnp.where(sub == 0, 1.0, 0.0)
    vt_aug = jnp.concatenate([vt[0:HEAD_DIM], ones_row, vt[HEAD_DIM:2 * HEAD_DIM], ones_row], axis=0).astype(BF16)
    for c in range(vt_ref.shape[1]):
        vt_ref[0, c] = vt_aug[:, c * KEY_TILE:(c + 1) * KEY_TILE]
    sd_ref[0] = _silu(p[:, g + 2 * KV_W:2 * g + 2 * KV_W]).astype(BF16)


def _in_projection(x, mod, g_pre, w_in, cos_t, sin_t, qg, kg, bd, ln_g, ln_b, tm):
    bsz, s, d = x.shape
    nt = s // tm
    row = lambda width: pl.BlockSpec((1, tm, width), lambda b, i: (b, i, 0))
    full = lambda a: pl.BlockSpec(a.shape, lambda b, i: (0,) * a.ndim)
    act = lambda width: jax.ShapeDtypeStruct((bsz, s, width), BF16)
    g = GROUP_W
    return pl.pallas_call(
        _in_kernel,
        grid=(bsz, nt),
        in_specs=[
            row(d),
            pl.BlockSpec((1, 3, d), lambda b, i: (b, 0, 0)),
            full(g_pre), full(w_in),
            pl.BlockSpec((tm, LANES), lambda b, i: (i, 0)),
            pl.BlockSpec((tm, LANES), lambda b, i: (i, 0)),
            full(qg), full(kg), full(bd), full(ln_g), full(ln_b),
        ],
        out_specs=[row(g), row(g), row(g), row(g), row(g), row(g), row(g), row(KV_W),
                   pl.BlockSpec((1, tm // KEY_TILE, N_KV_HEADS * V_ROWS, KEY_TILE), lambda b, i: (b, i, 0, 0)),
                   row(g)],
        out_shape=[act(g), act(g), act(g), act(g), act(g), act(g), act(g), act(KV_W),
                   jax.ShapeDtypeStruct((bsz, s // KEY_TILE, N_KV_HEADS * V_ROWS, KEY_TILE), BF16),
                   act(g)],
        compiler_params=pltpu.CompilerParams(vmem_limit_bytes=VMEM_LIMIT_BYTES),
        name="in_projection",
    )(x, mod, g_pre, w_in, cos_t, sin_t, qg, kg, bd, ln_g, ln_b)


def _attn_kernel(q_ref, k_ref, vt_ref, sd_ref, o_ref, sa_ref, sb_ref, acc_ref, *, n_pairs):
    q = q_ref[0]
    tq = q.shape[0]
    tk = sa_ref.shape[0]
    lane = lax.broadcasted_iota(jnp.int32, (tq, LANES), 1)
    zero = jnp.zeros((), BF16)
    qall = jnp.concatenate(
        [jnp.where((lane // HEAD_DIM) == kv, q[:, grp * LANES:(grp + 1) * LANES], zero)
         for kv in range(N_KV_HEADS) for grp in range(N_Q_HEADS // N_KV_HEADS)], axis=0)
    half = (N_Q_HEADS // N_KV_HEADS) * tq

    def scores(j):
        start = pl.multiple_of(j * tk, tk)
        return lax.dot_general(k_ref[0, pl.ds(start, tk), :], qall, (((1,), (1,)), ((), ())),
                               preferred_element_type=F32)

    def update(m, s_ref, j):
        s = s_ref[...]
        m_new = jnp.maximum(m, jnp.max(s, axis=0, keepdims=True))
        alpha = jnp.exp2(m - m_new)
        p = jnp.exp2(s - m_new).astype(BF16)
        for kv in range(N_KV_HEADS):
            cols = slice(kv * half, (kv + 1) * half)
            pv = _dot(vt_ref[0, j, kv * V_ROWS:(kv + 1) * V_ROWS, :], p[:, cols])
            acc_ref[kv] = alpha[:, cols] * acc_ref[kv] + pv
        return m_new

    acc_ref[...] = jnp.zeros_like(acc_ref)
    m = jnp.full((1, N_Q_HEADS * tq), -1e30, F32)
    sa_ref[...] = scores(0)

    def body(j, m):
        sb_ref[...] = scores(2 * j + 1)
        m = update(m, sa_ref, 2 * j)
        sa_ref[...] = scores(2 * j + 2)
        return update(m, sb_ref, 2 * j + 1)

    m = lax.fori_loop(0, n_pairs, body, m)
    update(m, sa_ref, 2 * n_pairs)

    heads = []
    for kv in range(N_KV_HEADS):
        acc = acc_ref[kv]
        o_kv = acc[0:HEAD_DIM, :] / acc[HEAD_DIM:HEAD_DIM + 1, :]
        heads += [o_kv[:, grp * tq:(grp + 1) * tq] for grp in range(N_Q_HEADS // N_KV_HEADS)]
    o = jnp.concatenate(heads, axis=0).T
    o_ref[0] = (o * sd_ref[0].astype(F32)).astype(BF16)


def _attention(q, k, vt, sd):
    bsz, s, g = q.shape
    n_tiles, v_rows, tk = vt.shape[1:]
    assert n_tiles % 2 == 1 and k.shape[1] == n_tiles * tk
    tq = min(Q_TILE, s)
    qspec = pl.BlockSpec((1, tq, g), lambda b, i: (b, i, 0))
    return pl.pallas_call(
        functools.partial(_attn_kernel, n_pairs=n_tiles // 2),
        grid=(bsz, s // tq),
        in_specs=[qspec,
                  pl.BlockSpec((1, k.shape[1], KV_W), lambda b, i: (b, 0, 0)),
                  pl.BlockSpec((1, n_tiles, v_rows, tk), lambda b, i: (b, 0, 0, 0)),
                  qspec],
        out_specs=qspec,
        out_shape=jax.ShapeDtypeStruct((bsz, s, g), BF16),
        scratch_shapes=[pltpu.VMEM((tk, N_Q_HEADS * tq), F32), pltpu.VMEM((tk, N_Q_HEADS * tq), F32),
                        pltpu.VMEM((N_KV_HEADS, V_ROWS, (N_Q_HEADS // N_KV_HEADS) * tq), F32)],
        compiler_params=pltpu.CompilerParams(vmem_limit_bytes=VMEM_LIMIT_BYTES),
        name="attention",
    )(q, k, vt, sd)


def _layer_norm(v, g, b):
    mu = jnp.mean(v, axis=-1, keepdims=True)
    vc = v - mu
    var = jnp.mean(vc * vc, axis=-1, keepdims=True)
    return vc * lax.rsqrt(var + LN_EPS) * g + b


def _out_kernel(t_ref, tp_ref, tn_ref, z_ref, zp_ref, zn_ref, pa_ref, sb_ref, vn_ref, uc_ref, yd_ref,
                x_ref, mod_ref, ca_ref, cb_ref, cbb_ref, clg_ref, clb_ref, wcat_ref, bias_ref, wout_ref, gpost_ref,
                o_ref, text_ref, zext_ref):
    i = pl.program_id(1)
    tm = t_ref.shape[1]
    has_prev = (i > 0).astype(F32)
    has_next = (i < pl.num_programs(1) - 1).astype(F32)

    def extend(ext_ref, main_ref, prev_ref, next_ref):
        ext_ref[0:HALO, :] = prev_ref[0].astype(F32) * has_prev
        ext_ref[HALO:HALO + tm, :] = main_ref[0].astype(F32)
        ext_ref[HALO + tm:2 * HALO + tm, :] = next_ref[0].astype(F32) * has_next

    def dwconv(ext_ref, w_ref):
        taps = w_ref.shape[0]
        first = HALO - taps // 2
        acc = ext_ref[first:first + tm, :] * w_ref[0:1, :]
        for k in range(1, taps):
            acc = acc + ext_ref[first + k:first + k + tm, :] * w_ref[k:k + 1, :]
        return acc

    extend(text_ref, t_ref, tp_ref, tn_ref)
    extend(zext_ref, z_ref, zp_ref, zn_ref)

    ya = pa_ref[0].astype(F32) * dwconv(text_ref, ca_ref)
    zc = dwconv(zext_ref, cb_ref) + cbb_ref[...]
    yb = _silu(_layer_norm(zc, clg_ref[...], clb_ref[...])) * sb_ref[0].astype(F32)

    lane_group = lax.broadcasted_iota(jnp.int32, (CHUNK, GROUP_W), 1) // (GROUP_W // N_SPATIAL_GROUPS)
    wcat = wcat_ref[...]
    bias = bias_ref[...]
    yc_chunks = []
    for c in range(tm // CHUNK):
        vchunk = vn_ref[0, c * CHUNK:(c + 1) * CHUNK, :]
        stacked = jnp.concatenate(
            [jnp.where(lane_group == grp, vchunk, jnp.zeros((), BF16)) for grp in range(N_SPATIAL_GROUPS)], axis=0)
        sg = _dot(wcat, stacked) + bias
        yc_chunks.append(sg * uc_ref[0, c * CHUNK:(c + 1) * CHUNK, :].astype(F32))
    yc = jnp.concatenate(yc_chunks, axis=0)

    y = jnp.concatenate([ya.astype(BF16), yb.astype(BF16), yc.astype(BF16), yd_ref[0]], axis=1)
    out = _dot(y, wout_ref[...])
    ms = jnp.mean(out * out, axis=-1, keepdims=True)
    yn = out * lax.rsqrt(ms + RMS_EPS) * gpost_ref[...]
    o_ref[0] = x_ref[0] + mod_ref[0, 2:3, :] * yn


def _mixer_output(t, pa, z, sb, vn, uc, yd, x, mod, conv_a, conv_b, conv_b_bias, conf_ln_g, conf_ln_b,
                  wcat, bias2d, w_out, g_post, tm):
    bsz, s, d = x.shape
    nt = s // tm
    hb = tm // HALO
    last_hb = s // HALO - 1
    row = lambda width: pl.BlockSpec((1, tm, width), lambda b, i: (b, i, 0))
    prev = pl.BlockSpec((1, HALO, GROUP_W), lambda b, i: (b, jnp.maximum(i * hb - 1, 0), 0))
    nxt = pl.BlockSpec((1, HALO, GROUP_W), lambda b, i: (b, jnp.minimum((i + 1) * hb, last_hb), 0))
    full = lambda a: pl.BlockSpec(a.shape, lambda b, i: (0,) * a.ndim)
    g = GROUP_W
    return pl.pallas_call(
        _out_kernel,
        grid=(bsz, nt),
        in_specs=[row(g), prev, nxt, row(g), prev, nxt, row(g), row(g), row(g), row(g), row(g),
                  row(d), pl.BlockSpec((1, 3, d), lambda b, i: (b, 0, 0)),
                  full(conv_a), full(conv_b), full(conv_b_bias), full(conf_ln_g), full(conf_ln_b),
                  full(wcat), full(bias2d), full(w_out), full(g_post)],
        out_specs=row(d),
        out_shape=jax.ShapeDtypeStruct((bsz, s, d), F32),
        scratch_shapes=[pltpu.VMEM((tm + 2 * HALO, g), F32), pltpu.VMEM((tm + 2 * HALO, g), F32)],
        compiler_params=pltpu.CompilerParams(vmem_limit_bytes=VMEM_LIMIT_BYTES),
        name="mixer_output",
    )(t, t, t, z, z, z, pa, sb, vn, uc, yd, x, mod, conv_a, conv_b, conv_b_bias, conf_ln_g, conf_ln_b,
      wcat, bias2d, w_out, g_post)


def _rope_tables(s):
    pos = jnp.arange(s)
    pos_row = (pos // GRID_W).astype(F32)
    pos_col = (pos % GRID_W).astype(F32)
    inv_freq = 1.0 / (ROPE_THETA ** (jnp.arange(0, AXIS_DIM, 2, dtype=F32) / AXIS_DIM))
    ang_r = pos_row[:, None] * inv_freq[None, :]
    ang_c = pos_col[:, None] * inv_freq[None, :]
    cos_h = jnp.concatenate([jnp.cos(ang_r), jnp.cos(ang_r), jnp.cos(ang_c), jnp.cos(ang_c)], axis=1)
    sin_h = jnp.concatenate([-jnp.sin(ang_r), jnp.sin(ang_r), -jnp.sin(ang_c), jnp.sin(ang_c)], axis=1)
    reps = LANES // HEAD_DIM
    return jnp.tile(cos_h, (1, reps)), jnp.tile(sin_h, (1, reps))


def kernel(x, c, ctx, c_ctx, w_mod, b_mod, g_pre, g_post, w_in, w_out, conv_a, conv_b, conv_b_bias, conf_ln_g,
           conf_ln_b, sgu_ln_g, sgu_ln_b, w_s, b_s, q_gain, k_gain):
    bsz, s, d = x.shape
    depth = w_mod.shape[0]
    ctx_len = ctx.shape[1]
    assert s % ROW_TILE == 0 and ROW_TILE % KEY_TILE == 0 and ctx_len == KEY_TILE and ctx_len % CHUNK == 0

    n_mod_rows = 8
    cc = jnp.concatenate([c, c_ctx[None, :], jnp.zeros((n_mod_rows - bsz - 1, d), F32)], axis=0)
    mods = _modulation(cc, w_mod, b_mod)

    cos_l, sin_l = _rope_tables(s)
    cos_c = jnp.ones((ctx_len, LANES), F32)
    sin_c = jnp.zeros((ctx_len, LANES), F32)
    head_id = jnp.arange(LANES) // HEAD_DIM
    bd = jnp.where(head_id[:, None] == head_id[None, :], 1.0 / HEAD_DIM, 0.0).astype(BF16)

    head_order = jnp.array([0, 2, 1, 3])
    q_cols = (COL_D + head_order[:, None] * HEAD_DIM + jnp.arange(HEAD_DIM)[None, :]).reshape(-1)
    cols = jnp.concatenate([jnp.arange(COL_D), q_cols, jnp.arange(COL_D + GROUP_W, PROJ_W)])
    w_in_b = jnp.take(w_in, cols, axis=2).astype(BF16)
    w_out_b = w_out.astype(BF16)
    reps = LANES // HEAD_DIM
    wcat = jnp.transpose(w_s, (0, 2, 1, 3)).reshape(depth, CHUNK, N_SPATIAL_GROUPS * CHUNK).astype(BF16)
    bias2d = jnp.repeat(jnp.transpose(b_s, (0, 2, 1)), GROUP_W // N_SPATIAL_GROUPS, axis=2)

    xc = ctx
    for l in range(depth):
        last = l == depth - 1
        mod_l = mods[l, :bsz].reshape(bsz, 3, d)
        mod_c = jnp.broadcast_to(mods[l, bsz].reshape(1, 3, d), (bsz, 3, d))
        qg = jnp.tile(q_gain[l], reps)[None, :]
        kg = jnp.tile(k_gain[l], reps)[None, :]
        shared = (qg, kg, bd, sgu_ln_g[l][None, :], sgu_ln_b[l][None, :])
        gp = g_pre[l][None, :]

        lat = _in_projection(x, mod_l, gp, w_in_b[l], cos_l, sin_l, *shared, tm=ROW_TILE)
        cpr = _in_projection(xc, mod_c, gp, w_in_b[l], cos_c, sin_c, *shared, tm=ctx_len)
        t, pa, z, sb, vn, uc, q, k, vt, sd = lat
        t_c, pa_c, z_c, sb_c, vn_c, uc_c, q_c, k_c, vt_c, sd_c = cpr

        mix = (conv_a[l], conv_b[l], conv_b_bias[l][None, :], conf_ln_g[l][None, :], conf_ln_b[l][None, :],
               wcat[l], bias2d[l], w_out_b[l], g_post[l][None, :])
        yd = _attention(q, jnp.concatenate([k_c, k], axis=1), jnp.concatenate([vt_c, vt], axis=1), sd)
        x = _mixer_output(t, pa, z, sb, vn, uc, yd, x, mod_l, *mix, tm=ROW_TILE)
        if not last:
            yd_c = _attention(q_c, k_c, vt_c, sd_c)
            xc = _mixer_output(t_c, pa_c, z_c, sb_c, vn_c, uc_c, yd_c, xc, mod_c, *mix, tm=ctx_len)
    return x
```

```python
import functools

import jax
import jax.numpy as jnp
from jax import lax
from jax.experimental import pallas as pl
from jax.experimental.pallas import tpu as pltpu

F32 = jnp.float32
BF16 = jnp.bfloat16

GRID_W = 64
GROUP_W = 256
HEAD_DIM = 64
N_Q_HEADS = 4
N_KV_HEADS = 2
KV_W = N_KV_HEADS * HEAD_DIM
AXIS_DIM = HEAD_DIM // 2
ROPE_THETA = 10000.0
ATTN_SCALE = HEAD_DIM ** -0.5
LOG2_E = 1.4426950408889634
Q_SCALE = ATTN_SCALE * LOG2_E
SCORE_BOUND_LIMIT = 32.0
SHORT_CONV_K = 3
CONFORMER_K = 31
CHUNK = 128
N_SPATIAL_GROUPS = 4
RMS_EPS = 1e-6
LN_EPS = 1e-5

COL_A = 0
COL_B = 4 * GROUP_W
COL_C = COL_B + 3 * GROUP_W
COL_D = COL_C + 3 * GROUP_W
PROJ_W = COL_D + 2 * GROUP_W + 2 * KV_W

LANES = 128
F32_SUBLANES = 8
BF16_SUBLANES = 16
HALO = BF16_SUBLANES
VMEM_LIMIT_BYTES = 56 * 1024 * 1024

ROW_TILE = 512
Q_TILE = 256
KEY_TILE = 256
BOUNDED_KEY_SUBTILES = 3
V_ROWS = HEAD_DIM + BF16_SUBLANES


def _silu(a):
    return a * jax.nn.sigmoid(a)


def _dot(a, b):
    return jnp.dot(a, b, preferred_element_type=F32)


def _split_bf16(a):
    hi = a.astype(BF16)
    lo = (a - hi.astype(F32)).astype(BF16)
    return hi, lo


def _mod_kernel(c_ref, w_ref, b_ref, o_ref):
    a_hi, a_lo = _split_bf16(_silu(c_ref[...]))
    w_hi, w_lo = _split_bf16(w_ref[0])
    o_ref[0] = _dot(a_hi, w_hi) + _dot(a_lo, w_hi) + _dot(a_hi, w_lo) + b_ref[0]


def _modulation(cc, w_mod, b_mod):
    depth, d, d3 = w_mod.shape
    tn = d
    return pl.pallas_call(
        _mod_kernel,
        grid=(depth, d3 // tn),
        in_specs=[
            pl.BlockSpec((cc.shape[0], d), lambda l, j: (0, 0)),
            pl.BlockSpec((1, d, tn), lambda l, j: (l, 0, j)),
            pl.BlockSpec((1, 1, tn), lambda l, j: (l, 0, j)),
        ],
        out_specs=pl.BlockSpec((1, cc.shape[0], tn), lambda l, j: (l, 0, j)),
        out_shape=jax.ShapeDtypeStruct((depth, cc.shape[0], d3), F32),
        compiler_params=pltpu.CompilerParams(vmem_limit_bytes=VMEM_LIMIT_BYTES),
        name="adaln_modulation",
    )(cc, w_mod, b_mod.reshape(depth, 1, d3))


def _in_kernel(x_ref, mod_ref, gpre_ref, w_ref, cos_ref, sin_ref, qg_ref, kg_ref, bd_ref, lng_ref, lnb_ref,
               t_ref, pa_ref, z_ref, sb_ref, vn_ref, uc_ref, q_ref, k_ref, vt_ref, sd_ref):
    x = x_ref[0]
    ms = jnp.mean(x * x, axis=-1, keepdims=True)
    y = x * lax.rsqrt(ms + RMS_EPS) * gpre_ref[...]
    sh = mod_ref[0, 0:1, :]
    sc = mod_ref[0, 1:2, :]
    h = (y * (1.0 + sc) + sh).astype(BF16)

    def proj(lo, width):
        return _dot(h, w_ref[:, lo:lo + width])

    g = GROUP_W
    p = proj(COL_A, 4 * g)
    t_ref[0] = (p[:, g:2 * g] * p[:, 2 * g:3 * g]).astype(BF16)
    pa_ref[0] = (p[:, 0:g] * _silu(p[:, 3 * g:4 * g])).astype(BF16)

    p = proj(COL_B, 3 * g)
    z_ref[0] = (p[:, 0:g] * jax.nn.sigmoid(p[:, g:2 * g])).astype(BF16)
    sb_ref[0] = _silu(p[:, 2 * g:3 * g]).astype(BF16)

    p = proj(COL_C, 3 * g)
    cv = p[:, g:2 * g]
    mu = jnp.mean(cv, axis=-1, keepdims=True)
    cvc = cv - mu
    var = jnp.mean(cvc * cvc, axis=-1, keepdims=True)
    vn_ref[0] = (cvc * lax.rsqrt(var + LN_EPS) * lng_ref[...] + lnb_ref[...]).astype(BF16)
    uc_ref[0] = (p[:, 0:g] * _silu(p[:, 2 * g:3 * g])).astype(BF16)

    p = proj(COL_D, 2 * g + 2 * KV_W)
    bd = bd_ref[...]
    cos = cos_ref[...]
    sin = sin_ref[...]
    lane = lax.broadcasted_iota(jnp.int32, (x.shape[0], LANES), 1)
    first_half = (lane % (2 * (AXIS_DIM // 2))) < (AXIS_DIM // 2)

    def norm_rope(u, gain):
        hi, lo = _split_bf16(u * u)
        msq = _dot(hi, bd) + _dot(lo, bd)
        u = u * lax.rsqrt(msq + RMS_EPS) * gain
        partner = jnp.where(first_half,
                            pltpu.roll(u, LANES - AXIS_DIM // 2, 1),
                            pltpu.roll(u, AXIS_DIM // 2, 1))
        return u * cos + partner * sin

    qg = qg_ref[...]
    q0 = norm_rope(p[:, 0:LANES], qg) * Q_SCALE
    q1 = norm_rope(p[:, LANES:2 * LANES], qg) * Q_SCALE
    q_ref[0] = jnp.concatenate([q0, q1], axis=1).astype(BF16)
    k_ref[0] = norm_rope(p[:, g:g + KV_W], kg_ref[...]).astype(BF16)
    vt = p[:, g + KV_W:g + 2 * KV_W].T
    sub = lax.broadcasted_iota(jnp.int32, (V_ROWS - HEAD_DIM, x.shape[0]), 0)
    ones_row = jnp.where(sub == 0, 1.0, 0.0)
    vt_aug = jnp.concatenate([vt[0:HEAD_DIM], ones_row, vt[HEAD_DIM:2 * HEAD_DIM], ones_row], axis=0).astype(BF16)
    for c in range(vt_ref.shape[1]):
        vt_ref[0, c] = vt_aug[:, c * KEY_TILE:(c + 1) * KEY_TILE]
    sd_ref[0] = _silu(p[:, g + 2 * KV_W:2 * g + 2 * KV_W]).astype(BF16)


def _in_projection(x, mod, g_pre, w_in, cos_t, sin_t, qg, kg, bd, ln_g, ln_b, tm):
    bsz, s, d = x.shape
    nt = s // tm
    row = lambda width: pl.BlockSpec((1, tm, width), lambda b, i: (b, i, 0))
    full = lambda a: pl.BlockSpec(a.shape, lambda b, i: (0,) * a.ndim)
    act = lambda width: jax.ShapeDtypeStruct((bsz, s, width), BF16)
    g = GROUP_W
    return pl.pallas_call(
        _in_kernel,
        grid=(bsz, nt),
        in_specs=[
            row(d),
            pl.BlockSpec((1, 3, d), lambda b, i: (b, 0, 0)),
            full(g_pre), full(w_in),
            pl.BlockSpec((tm, LANES), lambda b, i: (i, 0)),
            pl.BlockSpec((tm, LANES), lambda b, i: (i, 0)),
            full(qg), full(kg), full(bd), full(ln_g), full(ln_b),
        ],
        out_specs=[row(g), row(g), row(g), row(g), row(g), row(g), row(g), row(KV_W),
                   pl.BlockSpec((1, tm // KEY_TILE, N_KV_HEADS * V_ROWS, KEY_TILE), lambda b, i: (b, i, 0, 0)),
                   row(g)],
        out_shape=[act(g), act(g), act(g), act(g), act(g), act(g), act(g), act(KV_W),
                   jax.ShapeDtypeStruct((bsz, s // KEY_TILE, N_KV_HEADS * V_ROWS, KEY_TILE), BF16),
                   act(g)],
        compiler_params=pltpu.CompilerParams(vmem_limit_bytes=VMEM_LIMIT_BYTES),
        name="in_projection",
    )(x, mod, g_pre, w_in, cos_t, sin_t, qg, kg, bd, ln_g, ln_b)


def _stack_heads(q):
    lane = lax.broadcasted_iota(jnp.int32, (q.shape[0], LANES), 1)
    zero = jnp.zeros((), BF16)
    return jnp.concatenate(
        [jnp.where((lane // HEAD_DIM) == kv, q[:, grp * LANES:(grp + 1) * LANES], zero)
         for kv in range(N_KV_HEADS) for grp in range(N_Q_HEADS // N_KV_HEADS)], axis=0)


def _scores(k_ref, j, tk, qall):
    start = pl.multiple_of(j * tk, tk)
    return lax.dot_general(k_ref[0, pl.ds(start, tk), :], qall, (((1,), (1,)), ((), ())),
                           preferred_element_type=F32)


def _attn_finish(acc_ref, sd_ref, o_ref, tq):
    heads = []
    for kv in range(N_KV_HEADS):
        acc = acc_ref[kv]
        o_kv = acc[0:HEAD_DIM, :] / acc[HEAD_DIM:HEAD_DIM + 1, :]
        heads += [o_kv[:, grp * tq:(grp + 1) * tq] for grp in range(N_Q_HEADS // N_KV_HEADS)]
    o = jnp.concatenate(heads, axis=0).T
    o_ref[0] = (o * sd_ref[0].astype(F32)).astype(BF16)


def _attn_bounded_kernel(q_ref, k_ref, vt_ref, sd_ref, o_ref, sa_ref, sb_ref, acc_ref, *, n_pairs):
    tq = q_ref.shape[1]
    tk = sa_ref.shape[0]
    sub = tk // vt_ref.shape[3]
    qall = _stack_heads(q_ref[0])
    half = (N_Q_HEADS // N_KV_HEADS) * tq

    ck = vt_ref.shape[3]

    def step(src_ref, j_src, dst_ref, j_dst):
        pv = [None] * N_KV_HEADS
        for c in range(sub):
            rows = slice(c * ck, (c + 1) * ck)
            p = jnp.exp2(src_ref[rows, :])
            for kv in range(N_KV_HEADS):
                d = _dot(vt_ref[0, j_src * sub + c, kv * V_ROWS:(kv + 1) * V_ROWS, :], p[:, kv * half:(kv + 1) * half])
                pv[kv] = d if pv[kv] is None else pv[kv] + d
            if dst_ref is not None:
                dst_ref[rows, :] = _scores(k_ref, j_dst * sub + c, ck, qall).astype(BF16)
        for kv in range(N_KV_HEADS):
            acc_ref[kv] += pv[kv]

    acc_ref[...] = jnp.zeros_like(acc_ref)
    for c in range(sub):
        sa_ref[c * ck:(c + 1) * ck, :] = _scores(k_ref, c, ck, qall).astype(BF16)

    def body(j, carry):
        step(sa_ref, 2 * j, sb_ref, 2 * j + 1)
        step(sb_ref, 2 * j + 1, sa_ref, 2 * j + 2)
        return carry

    lax.fori_loop(0, n_pairs, body, 0)
    step(sa_ref, 2 * n_pairs, None, None)
    _attn_finish(acc_ref, sd_ref, o_ref, tq)


def _attn_kernel(q_ref, k_ref, vt_ref, sd_ref, o_ref, sa_ref, sb_ref, acc_ref, *, n_pairs):
    tq = q_ref.shape[1]
    tk = sa_ref.shape[0]
    qall = _stack_heads(q_ref[0])
    half = (N_Q_HEADS // N_KV_HEADS) * tq

    def scores(j):
        return _scores(k_ref, j, tk, qall)

    def update(m, s_ref, j):
        s = s_ref[...]
        m_new = jnp.maximum(m, jnp.max(s, axis=0, keepdims=True))
        alpha = jnp.exp2(m - m_new)
        p = jnp.exp2(s - m_new).astype(BF16)
        for kv in range(N_KV_HEADS):
            cols = slice(kv * half, (kv + 1) * half)
            pv = _dot(vt_ref[0, j, kv * V_ROWS:(kv + 1) * V_ROWS, :], p[:, cols])
            acc_ref[kv] = alpha[:, cols] * acc_ref[kv] + pv
        return m_new

    acc_ref[...] = jnp.zeros_like(acc_ref)
    m = jnp.full((1, N_Q_HEADS * tq), -1e30, F32)
    sa_ref[...] = scores(0)

    def body(j, m):
        sb_ref[...] = scores(2 * j + 1)
        m = update(m, sa_ref, 2 * j)
        sa_ref[...] = scores(2 * j + 2)
        return update(m, sb_ref, 2 * j + 1)

    m = lax.fori_loop(0, n_pairs, body, m)
    update(m, sa_ref, 2 * n_pairs)
    _attn_finish(acc_ref, sd_ref, o_ref, tq)


def _attention(q, k, vt, sd, score_bound):
    bsz, s, g = q.shape
    n_tiles, v_rows, tk = vt.shape[1:]
    assert n_tiles % 2 == 1 and k.shape[1] == n_tiles * tk
    tq = min(Q_TILE, s)
    qspec = pl.BlockSpec((1, tq, g), lambda b, i: (b, i, 0))
    acc = pltpu.VMEM((N_KV_HEADS, V_ROWS, (N_Q_HEADS // N_KV_HEADS) * tq), F32)
    common = dict(
        grid=(bsz, s // tq),
        in_specs=[qspec,
                  pl.BlockSpec((1, k.shape[1], KV_W), lambda b, i: (b, 0, 0)),
                  pl.BlockSpec((1, n_tiles, v_rows, tk), lambda b, i: (b, 0, 0, 0)),
                  qspec],
        out_specs=qspec,
        out_shape=jax.ShapeDtypeStruct((bsz, s, g), BF16),
        compiler_params=pltpu.CompilerParams(vmem_limit_bytes=VMEM_LIMIT_BYTES))
    sub = BOUNDED_KEY_SUBTILES if n_tiles % BOUNDED_KEY_SUBTILES == 0 else 1
    assert (n_tiles // sub) % 2 == 1
    bounded = pl.pallas_call(
        functools.partial(_attn_bounded_kernel, n_pairs=n_tiles // sub // 2),
        scratch_shapes=[pltpu.VMEM((sub * tk, N_Q_HEADS * tq), BF16), pltpu.VMEM((sub * tk, N_Q_HEADS * tq), BF16), acc],
        name="attention_bounded", **common)
    online = pl.pallas_call(
        functools.partial(_attn_kernel, n_pairs=n_tiles // 2),
        scratch_shapes=[pltpu.VMEM((tk, N_Q_HEADS * tq), F32), pltpu.VMEM((tk, N_Q_HEADS * tq), F32), acc],
        name="attention_online", **common)
    return lax.cond(score_bound <= SCORE_BOUND_LIMIT, bounded, online, q, k, vt, sd)


def _layer_norm(v, g, b):
    mu = jnp.mean(v, axis=-1, keepdims=True)
    vc = v - mu
    var = jnp.mean(vc * vc, axis=-1, keepdims=True)
    return vc * lax.rsqrt(var + LN_EPS) * g + b


def _out_kernel(t_ref, tp_ref, tn_ref, z_ref, zp_ref, zn_ref, pa_ref, sb_ref, vn_ref, uc_ref, yd_ref,
                x_ref, mod_ref, ca_ref, cb_ref, cbb_ref, clg_ref, clb_ref, wcat_ref, bias_ref, wout_ref, gpost_ref,
                o_ref, text_ref, zext_ref):
    i = pl.program_id(1)
    tm = t_ref.shape[1]
    has_prev = (i > 0).astype(F32)
    has_next = (i < pl.num_programs(1) - 1).astype(F32)

    def extend(ext_ref, main_ref, prev_ref, next_ref):
        ext_ref[0:HALO, :] = prev_ref[0].astype(F32) * has_prev
        ext_ref[HALO:HALO + tm, :] = main_ref[0].astype(F32)
        ext_ref[HALO + tm:2 * HALO + tm, :] = next_ref[0].astype(F32) * has_next

    def dwconv(ext_ref, w_ref):
        taps = w_ref.shape[0]
        first = HALO - taps // 2
        groups = {}
        for k in range(taps):
            base, residue = divmod(first + k, F32_SUBLANES)
            lo = base * F32_SUBLANES
            term = ext_ref[lo:lo + tm + F32_SUBLANES, :] * w_ref[k:k + 1, :]
            groups[residue] = term if residue not in groups else groups[residue] + term
        out = None
        for residue, part in groups.items():
            piece = part[residue:residue + tm, :]
            out = piece if out is None else out + piece
        return out

    extend(text_ref, t_ref, tp_ref, tn_ref)
    extend(zext_ref, z_ref, zp_ref, zn_ref)

    ya = pa_ref[0].astype(F32) * dwconv(text_ref, ca_ref)
    zc = dwconv(zext_ref, cb_ref) + cbb_ref[...]
    yb = _silu(_layer_norm(zc, clg_ref[...], clb_ref[...])) * sb_ref[0].astype(F32)

    lane_group = lax.broadcasted_iota(jnp.int32, (CHUNK, GROUP_W), 1) // (GROUP_W // N_SPATIAL_GROUPS)
    wcat = wcat_ref[...]
    bias = bias_ref[...]
    yc_chunks = []
    for c in range(tm // CHUNK):
        vchunk = vn_ref[0, c * CHUNK:(c + 1) * CHUNK, :]
        stacked = jnp.concatenate(
            [jnp.where(lane_group == grp, vchunk, jnp.zeros((), BF16)) for grp in range(N_SPATIAL_GROUPS)], axis=0)
        sg = _dot(wcat, stacked) + bias
        yc_chunks.append(sg * uc_ref[0, c * CHUNK:(c + 1) * CHUNK, :].astype(F32))
    yc = jnp.concatenate(yc_chunks, axis=0)

    y = jnp.concatenate([ya.astype(BF16), yb.astype(BF16), yc.astype(BF16), yd_ref[0]], axis=1)
    out = _dot(y, wout_ref[...])
    ms = jnp.mean(out * out, axis=-1, keepdims=True)
    yn = out * lax.rsqrt(ms + RMS_EPS) * gpost_ref[...]
    o_ref[0] = x_ref[0] + mod_ref[0, 2:3, :] * yn


def _mixer_output(t, pa, z, sb, vn, uc, yd, x, mod, conv_a, conv_b, conv_b_bias, conf_ln_g, conf_ln_b,
                  wcat, bias2d, w_out, g_post, tm):
    bsz, s, d = x.shape
    nt = s // tm
    hb = tm // HALO
    last_hb = s // HALO - 1
    row = lambda width: pl.BlockSpec((1, tm, width), lambda b, i: (b, i, 0))
    prev = pl.BlockSpec((1, HALO, GROUP_W), lambda b, i: (b, jnp.maximum(i * hb - 1, 0), 0))
    nxt = pl.BlockSpec((1, HALO, GROUP_W), lambda b, i: (b, jnp.minimum((i + 1) * hb, last_hb), 0))
    full = lambda a: pl.BlockSpec(a.shape, lambda b, i: (0,) * a.ndim)
    g = GROUP_W
    return pl.pallas_call(
        _out_kernel,
        grid=(bsz, nt),
        in_specs=[row(g), prev, nxt, row(g), prev, nxt, row(g), row(g), row(g), row(g), row(g),
                  row(d), pl.BlockSpec((1, 3, d), lambda b, i: (b, 0, 0)),
                  full(conv_a), full(conv_b), full(conv_b_bias), full(conf_ln_g), full(conf_ln_b),
                  full(wcat), full(bias2d), full(w_out), full(g_post)],
        out_specs=row(d),
        out_shape=jax.ShapeDtypeStruct((bsz, s, d), F32),
        scratch_shapes=[pltpu.VMEM((tm + 2 * HALO, g), F32), pltpu.VMEM((tm + 2 * HALO, g), F32)],
        compiler_params=pltpu.CompilerParams(vmem_limit_bytes=VMEM_LIMIT_BYTES),
        name="mixer_output",
    )(t, t, t, z, z, z, pa, sb, vn, uc, yd, x, mod, conv_a, conv_b, conv_b_bias, conf_ln_g, conf_ln_b,
      wcat, bias2d, w_out, g_post)


def _rope_tables(s):
    pos = jnp.arange(s)
    pos_row = (pos // GRID_W).astype(F32)
    pos_col = (pos % GRID_W).astype(F32)
    inv_freq = 1.0 / (ROPE_THETA ** (jnp.arange(0, AXIS_DIM, 2, dtype=F32) / AXIS_DIM))
    ang_r = pos_row[:, None] * inv_freq[None, :]
    ang_c = pos_col[:, None] * inv_freq[None, :]
    cos_h = jnp.concatenate([jnp.cos(ang_r), jnp.cos(ang_r), jnp.cos(ang_c), jnp.cos(ang_c)], axis=1)
    sin_h = jnp.concatenate([-jnp.sin(ang_r), jnp.sin(ang_r), -jnp.sin(ang_c), jnp.sin(ang_c)], axis=1)
    reps = LANES // HEAD_DIM
    return jnp.tile(cos_h, (1, reps)), jnp.tile(sin_h, (1, reps))


def kernel(x, c, ctx, c_ctx, w_mod, b_mod, g_pre, g_post, w_in, w_out, conv_a, conv_b, conv_b_bias, conf_ln_g,
           conf_ln_b, sgu_ln_g, sgu_ln_b, w_s, b_s, q_gain, k_gain):
    bsz, s, d = x.shape
    depth = w_mod.shape[0]
    ctx_len = ctx.shape[1]
    assert s % ROW_TILE == 0 and ROW_TILE % KEY_TILE == 0 and ctx_len == KEY_TILE and ctx_len % CHUNK == 0

    n_mod_rows = 8
    cc = jnp.concatenate([c, c_ctx[None, :], jnp.zeros((n_mod_rows - bsz - 1, d), F32)], axis=0)
    mods = _modulation(cc, w_mod, b_mod)

    cos_l, sin_l = _rope_tables(s)
    cos_c = jnp.ones((ctx_len, LANES), F32)
    sin_c = jnp.zeros((ctx_len, LANES), F32)
    head_id = jnp.arange(LANES) // HEAD_DIM
    bd = jnp.where(head_id[:, None] == head_id[None, :], 1.0 / HEAD_DIM, 0.0).astype(BF16)

    q_heads = [w_in[:, :, COL_D + h * HEAD_DIM:COL_D + (h + 1) * HEAD_DIM] for h in (0, 2, 1, 3)]
    w_in_b = jnp.concatenate([w_in[:, :, :COL_D]] + q_heads + [w_in[:, :, COL_D + GROUP_W:]], axis=2).astype(BF16)
    w_out_b = w_out.astype(BF16)
    reps = LANES // HEAD_DIM
    wcat = jnp.transpose(w_s, (0, 2, 1, 3)).reshape(depth, CHUNK, N_SPATIAL_GROUPS * CHUNK).astype(BF16)
    bias2d = jnp.repeat(jnp.transpose(b_s, (0, 2, 1)), GROUP_W // N_SPATIAL_GROUPS, axis=2)

    xc = ctx
    for l in range(depth):
        last = l == depth - 1
        mod_l = mods[l, :bsz].reshape(bsz, 3, d)
        mod_c = jnp.broadcast_to(mods[l, bsz].reshape(1, 3, d), (bsz, 3, d))
        qg = jnp.tile(q_gain[l], reps)[None, :]
        kg = jnp.tile(k_gain[l], reps)[None, :]
        shared = (qg, kg, bd, sgu_ln_g[l][None, :], sgu_ln_b[l][None, :])
        gp = g_pre[l][None, :]

        lat = _in_projection(x, mod_l, gp, w_in_b[l], cos_l, sin_l, *shared, tm=ROW_TILE)
        cpr = _in_projection(xc, mod_c, gp, w_in_b[l], cos_c, sin_c, *shared, tm=ctx_len)
        t, pa, z, sb, vn, uc, q, k, vt, sd = lat
        t_c, pa_c, z_c, sb_c, vn_c, uc_c, q_c, k_c, vt_c, sd_c = cpr

        mix = (conv_a[l], conv_b[l], conv_b_bias[l][None, :], conf_ln_g[l][None, :], conf_ln_b[l][None, :],
               wcat[l], bias2d[l], w_out_b[l], g_post[l][None, :])
        score_bound = 1.02 * Q_SCALE * HEAD_DIM * jnp.max(jnp.abs(q_gain[l])) * jnp.max(jnp.abs(k_gain[l]))
        yd = _attention(q, jnp.concatenate([k_c, k], axis=1), jnp.concatenate([vt_c, vt], axis=1), sd, score_bound)
        x = _mixer_output(t, pa, z, sb, vn, uc, yd, x, mod_l, *mix, tm=ROW_TILE)
        if not last:
            yd_c = _attention(q_c, k_c, vt_c, sd_c, score_bound)
            xc = _mixer_output(t_c, pa_c, z_c, sb_c, vn_c, uc_c, yd_c, xc, mod_c, *mix, tm=ctx_len)
    return x
```
